```python
import math
import jax, jax.numpy as jnp
from jax import lax
import numpy as np

D_MODEL = 1024
BATCH = 16
SEQ = 4096
DEPTH = 1
DEC_BATCH = 128
DEC_SEQ = 4
PAST_LEN = 8192
PAGE_SIZE = 128

D_MIX = D_MODEL
D_A = D_MIX // 4
G_A = 4
GD_A = D_A // G_A
CHUNK = 128
HEAD_DIM = 64
H_B = (D_MIX - D_A) // HEAD_DIM
D_B = H_B * HEAD_DIM
D_IN = 2 * D_A + 3 * D_B
DILATED = ((128, 1), (512, 4), (2048, 16))
WIN_MAX = max(w for w, _ in DILATED)
QBLK = 128
ROT_DIM = HEAD_DIM // 4
ROPE_THETA = 500000.0
PEER_HEADS = 8
N_KEYS = 128
N_EXPERTS = N_KEYS * N_KEYS
D_KEY = 256
DK_HALF = D_KEY // 2
TOPK = 16
TOK_BLK = 256
EPS = 1e-6
NEG = -1e30

kernel_name = "hybrid_gmlp_dilated_peer_step"


def rmsnorm(x, g):
    xf = x.astype(jnp.float32)
    y = xf * lax.rsqrt(jnp.mean(xf * xf, axis=-1, keepdims=True) + EPS)
    return y.astype(x.dtype) * g


def layernorm(x, g, b):
    xf = x.astype(jnp.float32)
    mu = jnp.mean(xf, axis=-1, keepdims=True)
    var = jnp.mean(jnp.square(xf - mu), axis=-1, keepdims=True)
    y = (xf - mu) * lax.rsqrt(var + EPS)
    return y.astype(x.dtype) * g + b


def rope_partial(x, pos):
    inv = ROPE_THETA ** (-jnp.arange(0, ROT_DIM, 2, dtype=jnp.float32) / ROT_DIM)
    ang = pos.astype(jnp.float32)[:, None] * inv[None, :]
    cos = jnp.cos(ang)[:, None, :]
    sin = jnp.sin(ang)[:, None, :]
    xr = x[..., :ROT_DIM].astype(jnp.float32)
    x1, x2 = xr[..., :ROT_DIM // 2], xr[..., ROT_DIM // 2:]
    rot = jnp.concatenate([x1 * cos - x2 * sin, x2 * cos + x1 * sin], axis=-1)
    return jnp.concatenate([rot.astype(x.dtype), x[..., ROT_DIM:]], axis=-1)


def mix_projection(h, w_in, ln_g, ln_b, pos):
    proj = h @ w_in
    lead = proj.shape[:-1]
    u = proj[..., :D_A]
    vn = layernorm(proj[..., D_A:2 * D_A], ln_g, ln_b)
    o = 2 * D_A
    q = proj[..., o:o + D_B].reshape(lead + (H_B, HEAD_DIM))
    k = proj[..., o + D_B:o + 2 * D_B].reshape(lead + (H_B, HEAD_DIM))
    v = proj[..., o + 2 * D_B:o + 3 * D_B].reshape(lead + (H_B, HEAD_DIM))
    q = rope_partial(q, pos) * (HEAD_DIM ** -0.5)
    k = rope_partial(k, pos)
    return u, vn, q, k, v


def gmlp_chunk_prompt(u, vn, w_s, b_s):
    B, S, _ = u.shape
    nc = S // CHUNK
    w = w_s * jnp.tril(jnp.ones((CHUNK, CHUNK), w_s.dtype))
    vr = vn.reshape(B, nc, CHUNK, G_A, GD_A)
    mixed = jnp.einsum('gij,bcjgd->bcigd', w, vr) + b_s.T[None, None, :, :, None]
    return u * mixed.reshape(B, S, D_A)


def gmlp_chunk_sample(u, vn, w_s, b_s):
    Bd, Sd, _ = u.shape
    w = (w_s * jnp.tril(jnp.ones((CHUNK, CHUNK), w_s.dtype)))[:, :Sd, :Sd]
    vr = vn.reshape(Bd, Sd, G_A, GD_A)
    mixed = jnp.einsum('gij,bjgd->bigd', w, vr) + b_s[:, :Sd].T[None, :, :, None]
    return u * mixed.reshape(Bd, Sd, D_A)


def dilated_branch_prompt(q, k, v, window, dil):
    B, S, H, HD = q.shape
    L = S // dil
    wn = window // dil
    nb = -(-L // QBLK)
    Lp = nb * QBLK

    def to_res(t):
        t = t.reshape(B, L, dil, H, HD).transpose(0, 2, 1, 3, 4)
        return jnp.pad(t, ((0, 0), (0, 0), (0, Lp - L), (0, 0), (0, 0)))

    def kblocks(t):
        t = jnp.pad(t, ((0, 0), (0, 0), (QBLK, 0), (0, 0), (0, 0)))
        t = t.reshape(B, dil, nb + 1, QBLK, H, HD)
        return jnp.concatenate([t[:, :, :-1], t[:, :, 1:]], axis=3)

    qb = to_res(q).reshape(B, dil, nb, QBLK, H, HD)
    kb = kblocks(to_res(k))
    vb = kblocks(to_res(v))
    s = jnp.einsum('brnqhd,brnkhd->brnhqk', qb, kb).astype(jnp.float32)
    qi = jnp.arange(QBLK)[None, :, None]
    kk = jnp.arange(2 * QBLK)[None, None, :]
    blk = jnp.arange(nb)[:, None, None]
    dist = QBLK + qi - kk
    key_idx = (blk - 1) * QBLK + kk
    valid = (dist >= 0) & (dist <= wn) & (key_idx >= 0)
    s = jnp.where(valid[None, None, :, None], s, NEG)
    m = jnp.max(s, axis=-1, keepdims=True)
    p = jnp.exp(s - m)
    den = jnp.sum(p, axis=-1)
    o = jnp.einsum('brnhqk,brnkhd->brnqhd', p.astype(v.dtype), vb)
    o = o / jnp.swapaxes(den, -1, -2)[..., None].astype(o.dtype)
    lse = m[..., 0] + jnp.log(den)
    o = o.reshape(B, dil, Lp, H, HD)[:, :, :L].transpose(0, 2, 1, 3, 4).reshape(B, S, H, HD)
    lse = jnp.swapaxes(lse, -1, -2).reshape(B, dil, Lp, H)[:, :, :L]
    lse = lse.transpose(0, 2, 1, 3).reshape(B, S, H)
    return o, lse


def dilated_branch_sample(q, k_all, v_all, window, dil, buf):
    Sd = q.shape[1]
    wn = window // dil
    idx = buf + jnp.arange(Sd)[:, None] - dil * jnp.arange(wn + 1)[None, :]
    valid = idx >= 0
    idc = jnp.clip(idx, 0, None)
    kg = k_all[:, idc]
    vg = v_all[:, idc]
    s = jnp.einsum('bqhd,bqkhd->bqhk', q, kg).astype(jnp.float32)
    s = jnp.where(valid[None, :, None, :], s, NEG)
    m = jnp.max(s, axis=-1, keepdims=True)
    p = jnp.exp(s - m)
    den = jnp.sum(p, axis=-1)
    o = jnp.einsum('bqhk,bqkhd->bqhd', p.astype(v_all.dtype), vg) / den[..., None].astype(v_all.dtype)
    lse = m[..., 0] + jnp.log(den)
    return o, lse


def combine_branches(outs, lses):
    alpha = jax.nn.softmax(jnp.stack(lses, axis=0), axis=0)
    o = jnp.sum(alpha[..., None].astype(outs[0].dtype) * jnp.stack(outs, axis=0), axis=0)
    return o.reshape(o.shape[:-2] + (D_B,))


def peer_ffn(x, w_q, sub_keys, u_tab, v_tab):
    lead = x.shape[:-1]
    xt = x.reshape(-1, D_MODEL)
    T = xt.shape[0]
    nblk = -(-T // TOK_BLK)
    xt = jnp.pad(xt, ((0, nblk * TOK_BLK - T), (0, 0))).reshape(nblk, TOK_BLK, D_MODEL)

    def block(xb):
        q = (xb @ w_q).reshape(TOK_BLK, PEER_HEADS, 2, DK_HALF)
        s = jnp.einsum('thcd,hcnd->thcn', q, sub_keys).astype(jnp.float32)
        s1, i1 = lax.top_k(s[:, :, 0], TOPK)
        s2, i2 = lax.top_k(s[:, :, 1], TOPK)
        cand = (s1[..., :, None] + s2[..., None, :]).reshape(TOK_BLK, PEER_HEADS, TOPK * TOPK)
        cid = (i1[..., :, None] * N_KEYS + i2[..., None, :]).reshape(TOK_BLK, PEER_HEADS, TOPK * TOPK)
        top_s, pos = lax.top_k(cand, TOPK)
        eid = jnp.take_along_axis(cid, pos, axis=-1)
        g = jax.nn.softmax(top_s, axis=-1).astype(xb.dtype)
        hid = jax.nn.gelu(jnp.einsum('thkd,td->thk', u_tab[eid], xb), approximate=False)
        return jnp.einsum('thk,thkd->td', g * hid, v_tab[eid])

    y = lax.map(block, xt).reshape(nblk * TOK_BLK, D_MODEL)[:T]
    return y.reshape(lead + (D_MODEL,))


def setup_inputs(seed: int = 0) -> dict:
    key = jax.random.key(seed)
    ks = jax.random.split(key, 20)
    buf = min(WIN_MAX, PAST_LEN)
    f32 = jnp.float32
    nrm = lambda k, shape, sc: jax.random.normal(k, shape, f32) * sc
    return {
        "x_prompt": nrm(ks[0], (BATCH, SEQ, D_MODEL), 1.0),
        "x_sample": nrm(ks[1], (DEC_BATCH, DEC_SEQ, D_MODEL), 1.0),
        "cache_k_win": nrm(ks[2], (DEPTH, DEC_BATCH, buf, H_B, HEAD_DIM), 1.0),
        "cache_v_win": nrm(ks[3], (DEPTH, DEC_BATCH, buf, H_B, HEAD_DIM), 1.0),
        "attn_norm": 1.0 + nrm(ks[4], (DEPTH, D_MODEL), 0.02),
        "w_in": nrm(ks[5], (DEPTH, D_MODEL, D_IN), D_MODEL ** -0.5),
        "ln_v_g": 1.0 + nrm(ks[6], (DEPTH, D_A), 0.02),
        "ln_v_b": nrm(ks[7], (DEPTH, D_A), 0.02),
        "w_s": nrm(ks[8], (DEPTH, G_A, CHUNK, CHUNK), CHUNK ** -0.5),
        "b_s": 1.0 + nrm(ks[9], (DEPTH, G_A, CHUNK), 0.1),
        "w_out": nrm(ks[10], (DEPTH, D_MIX, D_MODEL), D_MIX ** -0.5),
        "ffn_norm": 1.0 + nrm(ks[11], (DEPTH, D_MODEL), 0.02),
        "peer_w_q": nrm(ks[12], (DEPTH, D_MODEL, PEER_HEADS * D_KEY), D_MODEL ** -0.5),
        "peer_sub_keys": nrm(ks[13], (DEPTH, PEER_HEADS, 2, N_KEYS, DK_HALF), DK_HALF ** -0.5),
        "peer_u": nrm(ks[14], (DEPTH, N_EXPERTS, D_MODEL), D_MODEL ** -0.5),
        "peer_v": nrm(ks[15], (DEPTH, N_EXPERTS, D_MODEL), 0.25),
        "final_norm": 1.0 + nrm(ks[16], (D_MODEL,), 0.02),
    }


def reference(x_prompt, x_sample, cache_k_win, cache_v_win, attn_norm, w_in, ln_v_g, ln_v_b,
              w_s, b_s, w_out, ffn_norm, peer_w_q, peer_sub_keys, peer_u, peer_v, final_norm):
    S = x_prompt.shape[1]
    Sd = x_sample.shape[1]
    buf = cache_k_win.shape[2]
    buf_p = min(WIN_MAX, S)
    pos_p = jnp.arange(S)
    pos_s = PAST_LEN + jnp.arange(Sd)
    xp, xs = x_prompt, x_sample
    kp_list, vp_list, ks_list, vs_list, gv_list = [], [], [], [], []
    for l in range(DEPTH):
        h = rmsnorm(xp, attn_norm[l])
        u, vn, q, k, v = mix_projection(h, w_in[l], ln_v_g[l], ln_v_b[l], pos_p)
        a_out = gmlp_chunk_prompt(u, vn, w_s[l], b_s[l])
        res = [dilated_branch_prompt(q, k, v, w, r) for (w, r) in DILATED]
        b_out = combine_branches([o for o, _ in res], [s for _, s in res])
        xp = xp + jnp.concatenate([a_out, b_out], axis=-1) @ w_out[l]
        xp = xp + peer_ffn(rmsnorm(xp, ffn_norm[l]), peer_w_q[l], peer_sub_keys[l], peer_u[l], peer_v[l])
        kp_list.append(k[:, S - buf_p:])
        vp_list.append(v[:, S - buf_p:])
        h = rmsnorm(xs, attn_norm[l])
        u, vn, q, k, v = mix_projection(h, w_in[l], ln_v_g[l], ln_v_b[l], pos_s)
        a_out = gmlp_chunk_sample(u, vn, w_s[l], b_s[l])
        k_all = jnp.concatenate([cache_k_win[l].astype(k.dtype), k], axis=1)
        v_all = jnp.concatenate([cache_v_win[l].astype(v.dtype), v], axis=1)
        res = [dilated_branch_sample(q, k_all, v_all, w, r, buf) for (w, r) in DILATED]
        b_out = combine_branches([o for o, _ in res], [s for _, s in res])
        xs = xs + jnp.concatenate([a_out, b_out], axis=-1) @ w_out[l]
        xs = xs + peer_ffn(rmsnorm(xs, ffn_norm[l]), peer_w_q[l], peer_sub_keys[l], peer_u[l], peer_v[l])
        ks_list.append(k)
        vs_list.append(v)
        gv_list.append(vn)
    y_prompt = rmsnorm(xp, final_norm)
    y_sample = rmsnorm(xs, final_norm)
    k_win_prompt = jnp.stack(kp_list, axis=0)
    v_win_prompt = jnp.stack(vp_list, axis=0)
    k_win_sample_new = jnp.stack(ks_list, axis=0)
    v_win_sample_new = jnp.stack(vs_list, axis=0)
    gmlp_v_sample = jnp.stack(gv_list, axis=0)
    return (y_prompt, y_sample, k_win_prompt, v_win_prompt, k_win_sample_new, v_win_sample_new, gmlp_v_sample)
```

```python
import functools

import jax
import jax.numpy as jnp
from jax import lax
from jax.experimental import pallas as pl
from jax.experimental.pallas import tpu as pltpu

F32 = jnp.float32
BF16 = jnp.bfloat16

D_MODEL = 1024
D_A = 256
G_A = 4
GD_A = D_A // G_A
CHUNK = 128
HEAD_DIM = 64
H_B = 12
D_B = H_B * HEAD_DIM
DILATIONS = (1, 4, 16)
QBLK = 128
WIN_MAX = 2048
ROT_DIM = HEAD_DIM // 4
ROPE_THETA = 500000.0
PAST_LEN = 8192
PEER_HEADS = 8
N_KEYS = 128
N_EXPERTS = N_KEYS * N_KEYS
TOPK = 16
EPS = 1e-6
NEG = -1e30
NOT_SELECTED = 1e9

LANES = 128
V7X_VMEM_LIMIT_BYTES = 58 * 1024 * 1024

PROJ_TOKENS = 512
ROUTE_TOKENS = 256
PEER_TOKENS = 512
PEER_EXPERTS = 512


def _rmsnorm(x, g):
    return (x * lax.rsqrt(jnp.mean(x * x, axis=-1, keepdims=True) + EPS)) * g


def _dot(a, b):
    return jnp.dot(a, b, preferred_element_type=F32)


def _dot_nt(a, b):
    return lax.dot_general(a, b, (((1,), (1,)), ((), ())), preferred_element_type=F32)


def _proj_kernel(x_ref, g_ref, win_ref, lng_ref, lnb_ref, ws_ref, bias_ref, cos_ref, sa_ref, sb_ref,
                 a_ref, q_ref, k_ref, v_ref, *extra_refs, chunk):
    x = x_ref[...]
    hb = _rmsnorm(x, g_ref[...]).astype(BF16)

    def proj(lo, n):
        return _dot(hb, win_ref[:, lo:lo + n])

    u = proj(0, D_A)
    vr = proj(D_A, D_A)
    mu = jnp.mean(vr, axis=-1, keepdims=True)
    d = vr - mu
    var = jnp.mean(d * d, axis=-1, keepdims=True)
    vn = (d * lax.rsqrt(var + EPS)) * lng_ref[...] + lnb_ref[...]

    tokens = x.shape[0]
    group = lax.broadcasted_iota(jnp.int32, (chunk, D_A), 1) // GD_A
    for c in range(tokens // chunk):
        rows = slice(c * chunk, (c + 1) * chunk)
        vc = vn[rows]
        mixed = _dot(ws_ref[0], jnp.where(group == 0, vc, 0.0).astype(BF16))
        for g in range(1, G_A):
            mixed = mixed + _dot(ws_ref[g], jnp.where(group == g, vc, 0.0).astype(BF16))
        a_ref[rows, :] = (u[rows] * (mixed + bias_ref[...])).astype(BF16)

    cos, sa, sb = cos_ref[...], sa_ref[...], sb_ref[...]

    def rope(raw, lo):
        blk = raw[:, lo:lo + LANES]
        return blk * cos + pltpu.roll(blk, ROT_DIM // 2, 1) * sa + pltpu.roll(blk, LANES - ROT_DIM // 2, 1) * sb

    qr = proj(2 * D_A, D_B)
    kr = proj(2 * D_A + D_B, D_B)
    vv = proj(2 * D_A + 2 * D_B, D_B)
    for lo in range(0, D_B, LANES):
        q_ref[:, lo:lo + LANES] = rope(qr, lo) * (HEAD_DIM ** -0.5)
        k_ref[:, lo:lo + LANES] = rope(kr, lo)
    v_ref[...] = vv
    if extra_refs:
        extra_refs[0][...] = vn


def _rope_tables(pos):
    inv = ROPE_THETA ** (-jnp.arange(0, ROT_DIM, 2, dtype=F32) / ROT_DIM)
    ang = pos.astype(F32)[:, None] * inv[None, :]
    cos, sin = jnp.cos(ang), jnp.sin(ang)
    n = pos.shape[0]
    half = ROT_DIM // 2
    rest = HEAD_DIM - ROT_DIM
    c = jnp.concatenate([cos, cos, jnp.ones((n, rest), F32)], axis=1)
    sa = jnp.concatenate([jnp.zeros((n, half), F32), sin, jnp.zeros((n, rest), F32)], axis=1)
    sb = jnp.concatenate([-sin, jnp.zeros((n, half + rest), F32)], axis=1)
    two = lambda t: jnp.concatenate([t, t], axis=1)
    return two(c), two(sa), two(sb)


def _projection(x2d, attn_g, w_in_b, ln_g, ln_b, ws_b, bias, tables, *, chunk, table_tiles, emit_vn):
    tokens = x2d.shape[0]
    tt = PROJ_TOKENS
    assert tokens % tt == 0 and tt % chunk == 0
    d_in = w_in_b.shape[1]
    const = lambda shape: pl.BlockSpec(shape, lambda i: (0,) * len(shape))
    tab = pl.BlockSpec((tt, LANES), lambda i: (i % table_tiles, 0))
    tok = lambda w: pl.BlockSpec((tt, w), lambda i: (i, 0))
    out_shape = [jax.ShapeDtypeStruct((tokens, D_A), BF16)] + [jax.ShapeDtypeStruct((tokens, D_B), F32)] * 3
    out_specs = [tok(D_A), tok(D_B), tok(D_B), tok(D_B)]
    if emit_vn:
        out_shape.append(jax.ShapeDtypeStruct((tokens, D_A), F32))
        out_specs.append(tok(D_A))
    return pl.pallas_call(
        functools.partial(_proj_kernel, chunk=chunk),
        grid=(tokens // tt,),
        in_specs=[tok(D_MODEL), const((1, D_MODEL)), const((D_MODEL, d_in)), const((1, D_A)), const((1, D_A)),
                  const((G_A, chunk, chunk)), const((chunk, D_A)), tab, tab, tab],
        out_specs=out_specs,
        out_shape=out_shape,
        compiler_params=pltpu.CompilerParams(dimension_semantics=("arbitrary",),
                                             vmem_limit_bytes=V7X_VMEM_LIMIT_BYTES),
        name="projection",
    )(x2d, attn_g, w_in_b, ln_g, ln_b, ws_b, bias, *tables)


def _attn_kernel(q_ref, k_ref, v_ref, o_ref, o4_ref, l4_ref, o16_ref, l16_ref, *, seq):
    lane = lax.broadcasted_iota(jnp.int32, (QBLK, LANES), 1)
    head0 = lane < HEAD_DIM
    dist_prev = QBLK + lax.broadcasted_iota(jnp.int32, (QBLK, 2 * QBLK), 0) - lax.broadcasted_iota(
        jnp.int32, (QBLK, 2 * QBLK), 1)
    blocks = seq // QBLK

    def rows(start, size, r):
        return pl.ds(start, size) if r == 1 else pl.ds(start, size, stride=r)

    def one_block(r, c, first):
        rho = c % r
        n = c // r
        q_start = rho + r * QBLK * n
        k_start = q_start if first else q_start - r * QBLK
        dist = dist_prev - QBLK if first else dist_prev
        valid = (dist >= 0) & (dist <= QBLK)
        q = q_ref[rows(q_start, QBLK, r), :]
        kb = k_ref[rows(k_start, 2 * QBLK, r), :].astype(BF16)
        vb = v_ref[rows(k_start, 2 * QBLK, r), :].astype(BF16)
        outs, lses = [], []
        for h in range(2):
            qh = jnp.where(head0 if h == 0 else ~head0, q, 0.0).astype(BF16)
            s = jnp.where(valid, _dot_nt(qh, kb), NEG)
            m = jnp.max(s, axis=-1, keepdims=True)
            p = jnp.exp(s - m)
            den = jnp.sum(p, axis=-1, keepdims=True)
            outs.append(_dot(p.astype(BF16), vb) / den)
            lses.append(m + jnp.log(den))
        o = jnp.where(head0, outs[0], outs[1])
        lse = jnp.where(head0, jnp.broadcast_to(lses[0], o.shape), jnp.broadcast_to(lses[1], o.shape))
        return q_start, o, lse

    def branch_to_scratch(r, o_scr, l_scr):
        def store(c, first):
            q_start, o, lse = one_block(r, c, first)
            o_scr[rows(q_start, QBLK, r), :] = o
            l_scr[rows(q_start, QBLK, r), :] = lse

        def first_body(c, carry):
            store(c, True)
            return carry

        def rest_body(c, carry):
            store(c, False)
            return carry

        lax.fori_loop(0, r, first_body, 0)
        lax.fori_loop(r, blocks, rest_body, 0)

    branch_to_scratch(16, o16_ref, l16_ref)
    branch_to_scratch(4, o4_ref, l4_ref)

    def mix(c, first):
        q_start, o1, l1 = one_block(1, c, first)
        sl = pl.ds(q_start, QBLK)
        o4, l4, o16, l16 = o4_ref[sl, :], l4_ref[sl, :], o16_ref[sl, :], l16_ref[sl, :]
        top = jnp.maximum(jnp.maximum(l1, l4), l16)
        w1, w4, w16 = jnp.exp(l1 - top), jnp.exp(l4 - top), jnp.exp(l16 - top)
        tot = w1 + w4 + w16
        o_ref[sl, :] = ((w1 / tot) * o1 + (w4 / tot) * o4 + (w16 / tot) * o16).astype(BF16)

    def mix_body(c, carry):
        mix(c, False)
        return carry

    mix(0, True)
    lax.fori_loop(1, blocks, mix_body, 0)


def _prompt_attention(q, k, v):
    batch, seq, _ = q.shape
    assert seq % (QBLK * max(DILATIONS)) == 0 and seq // (QBLK * max(DILATIONS)) >= 2
    blk = pl.BlockSpec((None, seq, LANES), lambda b, hp: (b, 0, hp))
    scratch = [pltpu.VMEM((seq, LANES), F32)] * 4
    return pl.pallas_call(
        functools.partial(_attn_kernel, seq=seq),
        grid=(batch, D_B // LANES),
        in_specs=[blk, blk, blk],
        out_specs=blk,
        out_shape=jax.ShapeDtypeStruct((batch, seq, D_B), BF16),
        scratch_shapes=scratch,
        compiler_params=pltpu.CompilerParams(dimension_semantics=("arbitrary", "arbitrary"),
                                             vmem_limit_bytes=V7X_VMEM_LIMIT_BYTES),
        name="prompt_attention",
    )(q, k, v)


def _sample_attn_kernel(q_ref, kn_ref, vn_ref, kt_ref, vt_ref, o_ref, *, buf, new):
    rows8 = q_ref.shape[1]
    j = lax.broadcasted_iota(jnp.int32, (rows8, buf), 0)
    r = lax.broadcasted_iota(jnp.int32, (rows8, buf), 1)
    jq = lax.broadcasted_iota(jnp.int32, (rows8, rows8), 0)
    jk = lax.broadcasted_iota(jnp.int32, (rows8, rows8), 1)
    cache_masks, new_masks = [], []
    for dil in DILATIONS:
        back = buf + j - r
        cache_masks.append(((back & (dil - 1)) == 0) & (back <= dil * QBLK))
        nb = jq - jk
        new_masks.append((nb >= 0) & ((nb & (dil - 1)) == 0) & (jk < new))
    for h in range(H_B):
        qb = q_ref[h].astype(BF16)
        ktb = kt_ref[h].astype(BF16)
        vtb = vt_ref[h].astype(BF16)
        knb = kn_ref[h].astype(BF16)
        vnb = vn_ref[h].astype(BF16)
        s = _dot(qb, ktb)
        sn = _dot_nt(qb, knb)
        outs, lses = [], []
        for cm, nm in zip(cache_masks, new_masks):
            sm = jnp.where(cm, s, NEG)
            snm = jnp.where(nm, sn, NEG)
            m = jnp.maximum(jnp.max(sm, axis=-1, keepdims=True), jnp.max(snm, axis=-1, keepdims=True))
            p = jnp.exp(sm - m)
            pn = jnp.exp(snm - m)
            den = jnp.sum(p, axis=-1, keepdims=True) + jnp.sum(pn, axis=-1, keepdims=True)
            o = (_dot_nt(p.astype(BF16), vtb) + _dot(pn.astype(BF16), vnb)) / den
            outs.append(o)
            lses.append(m + jnp.log(den))
        top = jnp.maximum(jnp.maximum(lses[0], lses[1]), lses[2])
        ws = [jnp.exp(l - top) for l in lses]
        tot = ws[0] + ws[1] + ws[2]
        o_ref[h] = (ws[0] / tot) * outs[0] + (ws[1] / tot) * outs[1] + (ws[2] / tot) * outs[2]


def _sample_attention(qh, knh, vnh, kt, vt, *, new):
    batch, heads, rows8, hd = qh.shape
    buf = kt.shape[-1]
    small = pl.BlockSpec((None, heads, rows8, hd), lambda b: (b, 0, 0, 0))
    cache = pl.BlockSpec((None, heads, hd, buf), lambda b: (b, 0, 0, 0))
    return pl.pallas_call(
        functools.partial(_sample_attn_kernel, buf=buf, new=new),
        grid=(batch,),
        in_specs=[small, small, small, cache, cache],
        out_specs=small,
        out_shape=jax.ShapeDtypeStruct((batch, heads, rows8, hd), F32),
        compiler_params=pltpu.CompilerParams(dimension_semantics=("arbitrary",),
                                             vmem_limit_bytes=V7X_VMEM_LIMIT_BYTES),
        name="sample_attention",
    )(qh, knh, vnh, kt, vt)


def _top16(s):
    keys = lax.broadcasted_iota(jnp.int32, s.shape, 0).astype(F32)
    work = s
    rank = jnp.full(s.shape, NOT_SELECTED, F32)
    vals = []
    for k in range(TOPK):
        m = jnp.max(work, axis=0, keepdims=True)
        idx = jnp.min(jnp.where(work == m, keys, float(N_KEYS)), axis=0, keepdims=True)
        sel = keys == idx
        rank = jnp.where(sel, float(k), rank)
        work = jnp.where(sel, -jnp.inf, work)
        vals.append(m)
    return rank, vals


def _stack_rows(rows_list):
    n = len(rows_list)
    a_idx = lax.broadcasted_iota(jnp.int32, (n, rows_list[0].shape[1]), 0)
    out = jnp.broadcast_to(rows_list[0], a_idx.shape)
    for a in range(1, n):
        out = jnp.where(a_idx == a, rows_list[a], out)
    return out


def _staircase(v1_rows, v2_rows):
    v1 = _stack_rows(v1_rows)
    a_idx = lax.broadcasted_iota(jnp.int32, v1.shape, 0).astype(F32)
    count = jnp.zeros(v1.shape, F32)
    front = v1 + v2_rows[0]
    tops = []
    for step in range(TOPK):
        m = jnp.max(front, axis=0, keepdims=True)
        idx = jnp.min(jnp.where(front == m, a_idx, float(TOPK)), axis=0, keepdims=True)
        hit = a_idx == idx
        tops.append(m)
        if step == TOPK - 1:
            count = jnp.where(hit, count + 1.0, count)
            break
        taken = jnp.sum(jnp.where(hit, count, 0.0), axis=0, keepdims=True) + 1.0
        v1_hit = jnp.sum(jnp.where(hit, v1, 0.0), axis=0, keepdims=True)
        count = jnp.where(hit, count + 1.0, count)
        v2_next = jnp.full(taken.shape, -jnp.inf, F32)
        for b in range(1, TOPK):
            v2_next = jnp.where(taken == float(b), v2_rows[b], v2_next)
        front = jnp.where(hit, v1_hit + v2_next, front)
    return count, tops


def _route_kernel(x_ref, a_ref, b_ref, woa_ref, wob_ref, g_ref, wq_ref, sk_ref,
                  x2_ref, hnt_ref, nb_ref, e1_ref, r2_ref, e2_ref, q_scr):
    x2 = x_ref[...] + _dot(a_ref[...], woa_ref[...]) + _dot(b_ref[...], wob_ref[...])
    x2_ref[...] = x2
    hn = _rmsnorm(x2, g_ref[...])
    hnt_ref[...] = hn.T.astype(BF16)
    q = _dot(hn.astype(BF16), wq_ref[...])
    for hc in range(2 * PEER_HEADS):
        q_scr[hc] = q[:, hc * N_KEYS:(hc + 1) * N_KEYS].astype(BF16)

    def head(h, carry):
        s1 = _dot_nt(sk_ref[2 * h], q_scr[2 * h])
        s2 = _dot_nt(sk_ref[2 * h + 1], q_scr[2 * h + 1])
        rank1, v1_rows = _top16(s1)
        rank2, v2_rows = _top16(s2)
        count, tops = _staircase(v1_rows, v2_rows)
        nb = jnp.zeros(s1.shape, F32)
        for a in range(TOPK):
            nb = jnp.where(rank1 == float(a), count[a:a + 1, :], nb)
        z = jnp.exp(tops[0] - tops[0])
        for t in tops[1:]:
            z = z + jnp.exp(t - tops[0])
        nb_ref[h] = nb
        r2_ref[h] = rank2
        e1_ref[h] = jnp.exp(s1 - v1_rows[0])
        e2_ref[h] = jnp.exp(s2 - v2_rows[0]) / z
        return carry

    lax.fori_loop(0, PEER_HEADS, head, 0)


def _peer_route(x2d, a_out, b_out, wo_a, wo_b, ffn_g, wq_b, sk_b):
    tokens = x2d.shape[0]
    tt = ROUTE_TOKENS
    assert tokens % tt == 0
    const = lambda shape: pl.BlockSpec(shape, lambda i: (0,) * len(shape))
    tok = lambda w: pl.BlockSpec((tt, w), lambda i: (i, 0))
    per_key = pl.BlockSpec((PEER_HEADS, N_KEYS, tt), lambda i: (0, 0, i))
    per_key_shape = jax.ShapeDtypeStruct((PEER_HEADS, N_KEYS, tokens), F32)
    return pl.pallas_call(
        _route_kernel,
        grid=(tokens // tt,),
        in_specs=[tok(D_MODEL), tok(D_A), tok(D_B), const((D_A, D_MODEL)), const((D_B, D_MODEL)),
                  const((1, D_MODEL)), const((D_MODEL, 2 * PEER_HEADS * N_KEYS)),
                  const((2 * PEER_HEADS, N_KEYS, N_KEYS))],
        out_specs=[tok(D_MODEL), pl.BlockSpec((D_MODEL, tt), lambda i: (0, i)),
                   per_key, per_key, per_key, per_key],
        out_shape=[jax.ShapeDtypeStruct((tokens, D_MODEL), F32), jax.ShapeDtypeStruct((D_MODEL, tokens), BF16),
                   per_key_shape, per_key_shape, per_key_shape, per_key_shape],
        scratch_shapes=[pltpu.VMEM((2 * PEER_HEADS, tt, N_KEYS), BF16)],
        compiler_params=pltpu.CompilerParams(dimension_semantics=("arbitrary",),
                                             vmem_limit_bytes=V7X_VMEM_LIMIT_BYTES),
        name="peer_route",
    )(x2d, a_out, b_out, wo_a, wo_b, ffn_g, wq_b, sk_b)


def _peer_kernel(hnt_ref, u_ref, vt_ref, nb_ref, e1_ref, r2_ref, e2_ref, x2_ref, g_ref, y_ref, acc_ref):
    e = pl.program_id(1)

    @pl.when(e == 0)
    def _():
        acc_ref[...] = jnp.zeros_like(acc_ref)

    hid = _dot(u_ref[...], hnt_ref[...])
    act = 0.5 * hid * (1.0 + lax.erf(hid * (0.5 ** 0.5)))
    rows_per_step = PEER_EXPERTS // N_KEYS
    parts = []
    for jrow in range(rows_per_step):
        i1 = e * rows_per_step + jrow
        gate = jnp.zeros((N_KEYS, hid.shape[1]), F32)
        for h in range(PEER_HEADS):
            nb = nb_ref[h, pl.ds(i1, 1), :]
            e1 = e1_ref[h, pl.ds(i1, 1), :]
            gate = gate + jnp.where(r2_ref[h] < nb, e2_ref[h], 0.0) * e1
        parts.append((act[jrow * N_KEYS:(jrow + 1) * N_KEYS] * gate).astype(BF16))
    weighted = jnp.concatenate(parts, axis=0)
    acc_ref[...] += _dot(vt_ref[...], weighted)

    @pl.when(e == pl.num_programs(1) - 1)
    def _():
        y_ref[...] = _rmsnorm(x2_ref[...] + acc_ref[...].T, g_ref[...])


def _peer_dense(hnt, u_b, vt_b, nb, e1, r2, e2, x2, final_g):
    tokens = x2.shape[0]
    tt, te = PEER_TOKENS, PEER_EXPERTS
    assert tokens % tt == 0 and N_EXPERTS % te == 0 and te % N_KEYS == 0
    per_key = pl.BlockSpec((PEER_HEADS, N_KEYS, tt), lambda i, e: (0, 0, i))
    return pl.pallas_call(
        _peer_kernel,
        grid=(tokens // tt, N_EXPERTS // te),
        in_specs=[pl.BlockSpec((D_MODEL, tt), lambda i, e: (0, i)),
                  pl.BlockSpec((te, D_MODEL), lambda i, e: (e, 0)),
                  pl.BlockSpec((D_MODEL, te), lambda i, e: (0, e)),
                  per_key, per_key, per_key, per_key,
                  pl.BlockSpec((tt, D_MODEL), lambda i, e: (i, 0)),
                  pl.BlockSpec((1, D_MODEL), lambda i, e: (0, 0))],
        out_specs=pl.BlockSpec((tt, D_MODEL), lambda i, e: (i, 0)),
        out_shape=jax.ShapeDtypeStruct((tokens, D_MODEL), F32),
        scratch_shapes=[pltpu.VMEM((D_MODEL, tt), F32)],
        compiler_params=pltpu.CompilerParams(dimension_semantics=("arbitrary", "arbitrary"),
                                             vmem_limit_bytes=V7X_VMEM_LIMIT_BYTES),
        name="peer_dense",
    )(hnt, u_b, vt_b, nb, e1, r2, e2, x2, final_g)


def kernel(x_prompt, x_sample, cache_k_win, cache_v_win, attn_norm, w_in, ln_v_g, ln_v_b, w_s, b_s, w_out,
           ffn_norm, peer_w_q, peer_sub_keys, peer_u, peer_v, final_norm):
    batch, seq, _ = x_prompt.shape
    dec_batch, dec_seq, _ = x_sample.shape
    depth, _, buf = cache_k_win.shape[:3]
    assert depth == 1 and attn_norm.shape[0] == 1
    assert seq % PROJ_TOKENS == 0 and PROJ_TOKENS % CHUNK == 0 and seq >= WIN_MAX
    dec_tokens = dec_batch * dec_seq
    assert dec_tokens % PROJ_TOKENS == 0 and PROJ_TOKENS % dec_seq == 0 and dec_seq <= 8 and PAST_LEN % CHUNK == 0

    w_in_b = w_in[0].astype(BF16)
    attn_g, ffn_g, final_g = attn_norm[0][None], ffn_norm[0][None], final_norm[None]
    ln_g, ln_b = ln_v_g[0][None], ln_v_b[0][None]
    ws_tril = w_s[0] * jnp.tril(jnp.ones((CHUNK, CHUNK), w_s.dtype))
    wo_a, wo_b = w_out[0][:D_A].astype(BF16), w_out[0][D_A:].astype(BF16)
    wq_b = peer_w_q[0].astype(BF16)
    sk_b = peer_sub_keys[0].reshape(2 * PEER_HEADS, N_KEYS, N_KEYS).astype(BF16)
    u_b = peer_u[0].astype(BF16)
    vt_b = peer_v[0].T.astype(BF16)

    def ffn_and_norm(x2d, a_out, b_out):
        x2, hnt, nb, e1, r2, e2 = _peer_route(x2d, a_out, b_out, wo_a, wo_b, ffn_g, wq_b, sk_b)
        return _peer_dense(hnt, u_b, vt_b, nb, e1, r2, e2, x2, final_g)

    xp = x_prompt.reshape(batch * seq, D_MODEL)
    bias_p = jnp.repeat(b_s[0].T, GD_A, axis=1)
    a_p, q_p, k_p, v_p = _projection(
        xp, attn_g, w_in_b, ln_g, ln_b, ws_tril.astype(BF16), bias_p, _rope_tables(jnp.arange(seq)),
        chunk=CHUNK, table_tiles=seq // PROJ_TOKENS, emit_vn=False)
    to_bsd = lambda t: t.reshape(batch, seq, D_B)
    b_p = _prompt_attention(to_bsd(q_p), to_bsd(k_p), to_bsd(v_p)).reshape(batch * seq, D_B)
    y_prompt = ffn_and_norm(xp, a_p, b_p).reshape(batch, seq, D_MODEL)
    win = min(WIN_MAX, seq)
    k_win_prompt = to_bsd(k_p)[:, seq - win:].reshape(1, batch, win, H_B, HEAD_DIM)
    v_win_prompt = to_bsd(v_p)[:, seq - win:].reshape(1, batch, win, H_B, HEAD_DIM)

    xs = x_sample.reshape(dec_tokens, D_MODEL)
    reps = PROJ_TOKENS // dec_seq
    ws_s = jnp.einsum("rq,gij->griqj", jnp.eye(reps, dtype=F32), ws_tril[:, :dec_seq, :dec_seq])
    ws_s = ws_s.reshape(G_A, PROJ_TOKENS, PROJ_TOKENS).astype(BF16)
    bias_s = jnp.tile(jnp.repeat(b_s[0][:, :dec_seq].T, GD_A, axis=1), (reps, 1))
    tables_s = tuple(jnp.tile(t, (reps, 1)) for t in _rope_tables(PAST_LEN + jnp.arange(dec_seq)))
    a_s, q_s, k_s, v_s, vn_s = _projection(
        xs, attn_g, w_in_b, ln_g, ln_b, ws_s, bias_s, tables_s,
        chunk=PROJ_TOKENS, table_tiles=1, emit_vn=True)

    def to_heads(t):
        t = t.reshape(dec_batch, dec_seq, H_B, HEAD_DIM).transpose(0, 2, 1, 3)
        return jnp.pad(t, ((0, 0), (0, 0), (0, 8 - dec_seq), (0, 0)))

    kt = jnp.transpose(cache_k_win[0], (0, 2, 3, 1))
    vt = jnp.transpose(cache_v_win[0], (0, 2, 3, 1))
    o_s = _sample_attention(to_heads(q_s), to_heads(k_s), to_heads(v_s), kt, vt, new=dec_seq)
    b_s_out = o_s[:, :, :dec_seq].transpose(0, 2, 1, 3).reshape(dec_tokens, D_B).astype(BF16)
    y_sample = ffn_and_norm(xs, a_s, b_s_out).reshape(dec_batch, dec_seq, D_MODEL)

    k_win_sample_new = k_s.reshape(1, dec_batch, dec_seq, H_B, HEAD_DIM)
    v_win_sample_new = v_s.reshape(1, dec_batch, dec_seq, H_B, HEAD_DIM)
    gmlp_v_sample = vn_s.reshape(1, dec_batch, dec_seq, D_A)
    return (y_prompt, y_sample, k_win_prompt, v_win_prompt, k_win_sample_new, v_win_sample_new, gmlp_v_sample)
```
